```python
import jax, jax.numpy as jnp
from jax import lax
import numpy as np

D_MODEL = 1024
BATCH = 4
SEQ = 8192
DEPTH = 4
DEC_BATCH = 8
DEC_SEQ = 64
PAST_LEN = 1024

CHUNK = 64
QBLOCK = 128
H_A = 8
HD_A = 64
W_A = H_A * HD_A
H_IDX = 4
D_IDX = 64
TOPK_MAX = 256
H_B = 4
DK_B = 128
DV_B = 128
W_B = H_B * DV_B
W_C = D_MODEL // 2
CONV_W = 3
N_BRANCH = 3
D_FF = 4 * D_MODEL
ROPE_THETA = 10000.0
EPS = 1e-6
IN_SPLITS = (W_A, W_A, W_A, H_IDX * D_IDX, D_IDX, H_IDX,
             H_B * DK_B, H_B * DK_B, W_B, W_B,
             W_C, W_C, W_C, N_BRANCH * D_MODEL)
N_IN = 3 * W_A + H_IDX * D_IDX + D_IDX + H_IDX + 2 * H_B * DK_B + 2 * W_B + 3 * W_C + N_BRANCH * D_MODEL

kernel_name = 'streaming_hybrid_dsa_hgrn2_shortconv'


def rms_norm(x, g):
    xf = x.astype(jnp.float32)
    y = xf * lax.rsqrt(jnp.mean(xf * xf, axis=-1, keepdims=True) + EPS)
    return (y * g.astype(jnp.float32)).astype(x.dtype)


def rope(x, pos):
    half = x.shape[-1] // 2
    inv = ROPE_THETA ** (-jnp.arange(half, dtype=jnp.float32) / half)
    ang = pos.astype(jnp.float32)[:, None] * inv[None, :]
    ang = ang.reshape((ang.shape[0],) + (1,) * (x.ndim - 3) + (half,))
    cos, sin = jnp.cos(ang).astype(x.dtype), jnp.sin(ang).astype(x.dtype)
    x1, x2 = x[..., :half], x[..., half:]
    return jnp.concatenate([x1 * cos - x2 * sin, x2 * cos + x1 * sin], axis=-1)


def split_cols(z):
    idx = np.cumsum(np.array(IN_SPLITS))[:-1].tolist()
    return jnp.split(z, idx, axis=-1)


def dsa_attend(q, qi, wi, t_pos, k, v, ki, s_pos, topk):
    dots = jnp.einsum('bqhd,bsd->bqhs', qi, ki) * (D_IDX ** -0.5)
    idx_score = jnp.einsum('bqh,bqhs->bqs', wi, jax.nn.relu(dots)).astype(jnp.float32)
    admissible = (s_pos[None, :] // CHUNK) <= (t_pos[:, None] // CHUNK)
    idx_score = jnp.where(admissible[None], idx_score, -jnp.inf)
    _, top_idx = lax.top_k(idx_score, topk)
    valid = (s_pos[top_idx] // CHUNK) <= (t_pos[None, :, None] // CHUNK)
    kg = jax.vmap(lambda kb, ib: kb[ib])(k, top_idx)
    vg = jax.vmap(lambda vb, ib: vb[ib])(v, top_idx)
    logits = jnp.einsum('bqhd,bqkhd->bqhk', q, kg).astype(jnp.float32) * (HD_A ** -0.5)
    logits = jnp.where(valid[:, :, None, :], logits, -jnp.inf)
    p = jax.nn.softmax(logits, axis=-1).astype(v.dtype)
    return jnp.einsum('bqhk,bqkhd->bqhd', p, vg)


def dsa_prompt(q, qi, wi, pos, k, v, ki, topk):
    B, T = q.shape[0], q.shape[1]
    nblk = T // QBLOCK

    def blocks(a):
        return a.reshape((B, nblk, QBLOCK) + a.shape[2:]).swapaxes(0, 1)

    def one_block(xs):
        qb, qib, wib, tpos = xs
        return dsa_attend(qb, qib, wib, tpos, k, v, ki, pos, topk)

    out = lax.map(one_block, (blocks(q), blocks(qi), blocks(wi), pos.reshape(nblk, QBLOCK)))
    return out.swapaxes(0, 1).reshape(B, T, H_A, HD_A)


def hgrn2_recurrence(q, k, v, logf, s0):
    B, T, H, DK = q.shape
    DV = v.shape[-1]
    C = min(CHUNK, T)
    n = T // C
    tri = jnp.tril(jnp.ones((C, C), dtype=bool))[None, :, :, None, None]

    def to_chunks(a):
        return a.reshape((B, n, C) + a.shape[2:]).swapaxes(0, 1)

    def step(S, xs):
        qc, kc, vc, gc = xs
        b = jnp.cumsum(gc, axis=1)
        o_inter = jnp.einsum('bthk,bhkv->bthv', qc * jnp.exp(b), S)
        diff = b[:, :, None] - b[:, None, :]
        decay = jnp.where(tri, jnp.exp(jnp.where(tri, diff, 0.0)), 0.0)
        attn = jnp.einsum('bthk,btshk,bshk->bhts', qc, decay, kc)
        o_intra = jnp.einsum('bhts,bshv->bthv', attn, vc)
        b_last = b[:, -1]
        S_new = jnp.exp(b_last)[..., None] * S + jnp.einsum('bshk,bshv->bhkv', kc * jnp.exp(b_last[:, None] - b), vc)
        return S_new, o_inter + o_intra

    S_fin, o = lax.scan(step, s0, (to_chunks(q), to_chunks(k), to_chunks(v), to_chunks(logf)))
    return o.swapaxes(0, 1).reshape(B, T, H, DV), S_fin


def trunk_layer(x, pos, past, p):
    (norm1_g, w_in, q_norm_g, k_norm_g, lb, gnorm_g, conv_w,
     w_br_a, w_br_b, w_br_c, w_out, norm2_g, w_up, w_down) = p
    B, T, _ = x.shape
    h = rms_norm(x, norm1_g)
    (aq, ak, av, iq, ik, iw, bq, bf, bi, bg, cb, cc, cx, gates) = split_cols(h @ w_in)

    q = rope(rms_norm(aq.reshape(B, T, H_A, HD_A), q_norm_g), pos)
    k = rope(rms_norm(ak.reshape(B, T, H_A, HD_A), k_norm_g), pos)
    v = av.reshape(B, T, H_A, HD_A)
    qi = rope(iq.reshape(B, T, H_IDX, D_IDX), pos)
    ki = rope(ik, pos)
    wi = iw * (H_IDX ** -0.5)
    if past is None:
        ya = dsa_prompt(q, qi, wi, pos, k, v, ki, min(TOPK_MAX, T // 4))
        s0 = jnp.zeros((B, H_B, DK_B, DV_B), jnp.float32)
        buf = jnp.zeros((B, CONV_W - 1, W_C), x.dtype)
    else:
        past_k, past_v, past_ki, s0, buf = past
        L = past_k.shape[1] + T
        k_all = jnp.concatenate([past_k.astype(k.dtype), k], axis=1)
        v_all = jnp.concatenate([past_v.astype(v.dtype), v], axis=1)
        ki_all = jnp.concatenate([past_ki.astype(ki.dtype), ki], axis=1)
        ya = dsa_attend(q, qi, wi, pos, k_all, v_all, ki_all, jnp.arange(L), min(TOPK_MAX, L // 4))

    q_b = jax.nn.silu(bq.astype(jnp.float32)).reshape(B, T, H_B, DK_B) * (DK_B ** -0.5)
    f_pre = bf.astype(jnp.float32).reshape(B, T, H_B, DK_B)
    lbh = lb.reshape(H_B, DK_B)
    logf = jnp.logaddexp(jnp.log(lbh), jnp.log1p(-lbh) + jax.nn.log_sigmoid(f_pre))
    k_b = (1.0 - lbh) * jax.nn.sigmoid(-f_pre)
    v_b = bi.astype(jnp.float32).reshape(B, T, H_B, DV_B)
    o_b, s_new = hgrn2_recurrence(q_b, k_b, v_b, logf, s0.astype(jnp.float32))
    yb = (rms_norm(o_b, gnorm_g) * jax.nn.silu(bg.astype(jnp.float32).reshape(B, T, H_B, DV_B)))
    yb = yb.astype(x.dtype).reshape(B, T, W_B)

    u = cc * cx
    u_pad = jnp.concatenate([buf.astype(u.dtype), u], axis=1)
    conv = sum(u_pad[:, j:j + T] * conv_w[j] for j in range(CONV_W))
    yc = cb * conv
    new_buf = u_pad[:, T:]

    g_a, g_b, g_c = jnp.split(jax.nn.sigmoid(gates), N_BRANCH, axis=-1)
    merged = g_a * (ya.reshape(B, T, W_A) @ w_br_a) + g_b * (yb @ w_br_b) + g_c * (yc @ w_br_c)
    x = x + merged @ w_out
    h2 = rms_norm(x, norm2_g)
    x = x + jnp.square(jax.nn.relu(h2 @ w_up)) @ w_down
    return x, (k, v, ki, s_new, new_buf)


def setup_inputs(seed: int = 0) -> dict:
    key = jax.random.key(seed)
    ks = jax.random.split(key, 21)

    def nrm(k, shape, scale):
        return jax.random.normal(k, shape, jnp.float32) * scale

    return {
        'x_prompt': nrm(ks[0], (BATCH, SEQ, D_MODEL), 1.0),
        'x_sample': nrm(ks[1], (DEC_BATCH, DEC_SEQ, D_MODEL), 1.0),
        'cache_k': nrm(ks[2], (DEPTH, DEC_BATCH, PAST_LEN, H_A, HD_A), 1.0),
        'cache_v': nrm(ks[3], (DEPTH, DEC_BATCH, PAST_LEN, H_A, HD_A), 1.0),
        'cache_kidx': nrm(ks[4], (DEPTH, DEC_BATCH, PAST_LEN, D_IDX), 1.0),
        'state_hgrn': nrm(ks[5], (DEPTH, DEC_BATCH, H_B, DK_B, DV_B), 0.3),
        'state_conv': nrm(ks[6], (DEPTH, DEC_BATCH, CONV_W - 1, W_C), 1.0),
        'norm1_g': 1.0 + nrm(ks[7], (DEPTH, D_MODEL), 0.01),
        'w_in': nrm(ks[8], (DEPTH, D_MODEL, N_IN), D_MODEL ** -0.5),
        'q_norm_g': 1.0 + nrm(ks[9], (DEPTH, HD_A), 0.01),
        'k_norm_g': 1.0 + nrm(ks[10], (DEPTH, HD_A), 0.01),
        'hgrn_lb_logits': nrm(ks[11], (DEPTH, H_B * DK_B), 0.1),
        'hgrn_gnorm_g': 1.0 + nrm(ks[12], (DEPTH, DV_B), 0.01),
        'conv_w': nrm(ks[13], (DEPTH, CONV_W, W_C), 0.5),
        'w_br_a': nrm(ks[14], (DEPTH, W_A, D_MODEL), W_A ** -0.5),
        'w_br_b': nrm(ks[15], (DEPTH, W_B, D_MODEL), W_B ** -0.5),
        'w_br_c': nrm(ks[16], (DEPTH, W_C, D_MODEL), W_C ** -0.5),
        'w_out': nrm(ks[17], (DEPTH, D_MODEL, D_MODEL), D_MODEL ** -0.5),
        'norm2_g': 1.0 + nrm(ks[18], (DEPTH, D_MODEL), 0.01),
        'w_up': nrm(ks[19], (DEPTH, D_MODEL, D_FF), D_MODEL ** -0.5),
        'w_down': nrm(ks[20], (DEPTH, D_FF, D_MODEL), D_FF ** -0.5),
    }


def reference(x_prompt, x_sample, cache_k, cache_v, cache_kidx, state_hgrn, state_conv,
              norm1_g, w_in, q_norm_g, k_norm_g, hgrn_lb_logits, hgrn_gnorm_g, conv_w,
              w_br_a, w_br_b, w_br_c, w_out, norm2_g, w_up, w_down):
    lb_cum = jnp.cumsum(jax.nn.softmax(hgrn_lb_logits.astype(jnp.float32), axis=0), axis=0)
    lower_bounds = lb_cum - lb_cum[:1]
    pos_p = jnp.arange(x_prompt.shape[1])
    pos_s = cache_k.shape[2] + jnp.arange(x_sample.shape[1])
    xp, xs = x_prompt, x_sample
    new_p, new_s = [], []
    for l in range(DEPTH):
        p = (norm1_g[l], w_in[l], q_norm_g[l], k_norm_g[l], lower_bounds[l], hgrn_gnorm_g[l], conv_w[l],
             w_br_a[l], w_br_b[l], w_br_c[l], w_out[l], norm2_g[l], w_up[l], w_down[l])
        xp, st_p = trunk_layer(xp, pos_p, None, p)
        xs, st_s = trunk_layer(xs, pos_s, (cache_k[l], cache_v[l], cache_kidx[l], state_hgrn[l], state_conv[l]), p)
        new_p.append(st_p)
        new_s.append(st_s)

    def stack(lst, i):
        return jnp.stack([s[i] for s in lst], axis=0)

    return (xp, xs,
            stack(new_p, 0), stack(new_p, 1), stack(new_p, 2), stack(new_p, 3), stack(new_p, 4),
            stack(new_s, 0), stack(new_s, 1), stack(new_s, 2), stack(new_s, 3), stack(new_s, 4))
```

```python
import functools

import numpy as np
import jax
import jax.numpy as jnp
from jax import lax
from jax.experimental import pallas as pl
from jax.experimental.pallas import tpu as pltpu

CHUNK = 64
H_A = 8
HD_A = 64
W_A = H_A * HD_A
H_IDX = 4
D_IDX = 64
TOPK_MAX = 256
H_B = 4
DK_B = 128
DV_B = 128
W_B = H_B * DV_B
CONV_W = 3
ROPE_THETA = 10000.0
EPS = 1e-6

F32 = jnp.float32
BF16 = jnp.bfloat16
INT_MIN = np.int32(-2 ** 31)
LANE = 128
VMEM_LIMIT = 56 * 1024 * 1024


def _cparams(sem):
    return pltpu.CompilerParams(dimension_semantics=sem, vmem_limit_bytes=VMEM_LIMIT)


def _norm_mm_kernel(x_ref, g_ref, w_ref, o_ref, h_ref):
    @pl.when(pl.program_id(1) == 0)
    def _():
        x = x_ref[...]
        ms = jnp.mean(x * x, axis=-1, keepdims=True)
        h_ref[...] = (x * lax.rsqrt(ms + EPS) * g_ref[...]).astype(BF16)

    o_ref[...] = jnp.dot(h_ref[...], w_ref[...], preferred_element_type=F32)


def norm_matmul(x, g, w, *, tm, tn):
    m, d = x.shape
    n = w.shape[1]
    return pl.pallas_call(
        _norm_mm_kernel,
        grid=(m // tm, n // tn),
        in_specs=[pl.BlockSpec((tm, d), lambda i, j: (i, 0)),
                  pl.BlockSpec((1, d), lambda i, j: (0, 0)),
                  pl.BlockSpec((d, tn), lambda i, j: (0, j))],
        out_specs=pl.BlockSpec((tm, tn), lambda i, j: (i, j)),
        out_shape=jax.ShapeDtypeStruct((m, n), F32),
        scratch_shapes=[pltpu.VMEM((tm, d), BF16)],
        compiler_params=_cparams(("parallel", "arbitrary")),
        name="norm_matmul",
    )(x, g, w)


def _dsa_kernel(qiT_ref, wiT_ref, ki_ref, qTz_ref, k_ref, vT_ref, tri_ref, o_ref,
                key_ref, acc_ref, m_ref, l_ref, *, tq, kb, q_pos0, s_valid, topk):
    i = pl.program_id(1)
    t0 = q_pos0 + i * tq
    lane = lax.broadcasted_iota(jnp.int32, (1, tq), 1)
    qchunk = (t0 + lane) // CHUNK
    n_adm = jnp.minimum(((t0 + tq - 1) // CHUNK + 1) * CHUNK, s_valid)
    nkb = (n_adm + kb - 1) // kb
    row = lax.broadcasted_iota(jnp.int32, (kb, 1), 0)

    wi = wiT_ref[0]

    def score_body(j, c):
        off = pl.multiple_of(j * kb, kb)
        kib = ki_ref[0, pl.ds(off, kb), :]
        sc = jnp.zeros((kb, tq), F32)
        for h in range(H_IDX):
            d = jnp.dot(kib, qiT_ref[0, h * D_IDX:(h + 1) * D_IDX, :],
                        preferred_element_type=F32)
            sc = sc + wi[h:h + 1, :] * jnp.maximum(d, 0.0)
        bits = pltpu.bitcast(sc, jnp.int32)
        key = bits ^ ((bits >> 31) & jnp.int32(0x7FFFFFFF))
        key = jnp.where(sc == 0.0, 0, key)
        s_idx = off + row
        adm = ((s_idx // CHUNK) <= qchunk) & (s_idx < s_valid)
        key_ref[j] = jnp.where(adm, key, INT_MIN)
        return c

    lax.fori_loop(0, nkb, score_body, 0)

    def count(pred):
        def blk(j, cnt):
            hit = jnp.where(pred(key_ref[j]), 1, 0)
            return cnt + jnp.sum(hit.reshape(kb // 8, 8, tq), axis=0)
        cnt = lax.fori_loop(0, nkb, blk, jnp.zeros((8, tq), jnp.int32))
        return jnp.sum(cnt, axis=0, keepdims=True)

    def bit_body(b, tu):
        cand_u = tu | jnp.left_shift(jnp.int32(1), 31 - b)
        cand = cand_u ^ INT_MIN
        c = count(lambda kv: kv >= cand)
        return jnp.where(c >= topk, cand_u, tu)

    tu = lax.fori_loop(0, 32, bit_body, jnp.zeros((1, tq), jnp.int32))
    thr = jnp.maximum(tu ^ INT_MIN, INT_MIN + 1)
    n_gt = count(lambda kv: kv > thr)
    n_tie = (topk - n_gt).astype(F32)

    m_ref[...] = jnp.full((H_A, tq), -1e30, F32)
    l_ref[...] = jnp.zeros((H_A, tq), F32)
    acc_ref[...] = jnp.zeros((W_A, tq), F32)

    def att_body(j, carry):
        off = pl.multiple_of(j * kb, kb)
        kv = key_ref[j]
        eq = kv == thr
        rank = jnp.dot(tri_ref[...], jnp.where(eq, 1.0, 0.0).astype(BF16),
                       preferred_element_type=F32) + carry
        sel = (kv > thr) | (eq & (rank <= n_tie))
        kblk = k_ref[0, pl.ds(off, kb), :]
        for h in range(H_A):
            g = h // 2
            lg = jnp.dot(kblk[:, g * LANE:(g + 1) * LANE], qTz_ref[0, h * LANE:(h + 1) * LANE, :],
                         preferred_element_type=F32)
            lg = jnp.where(sel, lg, -jnp.inf)
            m_old = m_ref[h:h + 1, :]
            m_new = jnp.maximum(m_old, jnp.max(lg, axis=0, keepdims=True))
            alpha = jnp.exp(m_old - m_new)
            p = jnp.exp(lg - m_new)
            l_ref[h:h + 1, :] = alpha * l_ref[h:h + 1, :] + jnp.sum(p, axis=0, keepdims=True)
            m_ref[h:h + 1, :] = m_new
            pv = jnp.dot(vT_ref[0, j, h * HD_A:(h + 1) * HD_A, :], p.astype(BF16),
                         preferred_element_type=F32)
            acc_ref[h * HD_A:(h + 1) * HD_A, :] = alpha * acc_ref[h * HD_A:(h + 1) * HD_A, :] + pv
        return rank[kb - 1:kb, :]

    lax.fori_loop(0, nkb, att_body, jnp.zeros((1, tq), F32))

    inv = 1.0 / l_ref[...]
    outT = jnp.concatenate(
        [acc_ref[h * HD_A:(h + 1) * HD_A, :] * inv[h:h + 1, :] for h in range(H_A)], axis=0)
    o_ref[0] = outT.T.astype(o_ref.dtype)


def dsa_attention(qiT, wiT, ki, qTz, k, vT4, *, q_pos0, s_valid, topk, tq, kb):
    b, _, t = qiT.shape
    s = ki.shape[1]
    tri = jnp.tril(jnp.ones((kb, kb), BF16))
    kern = functools.partial(_dsa_kernel, tq=tq, kb=kb, q_pos0=q_pos0, s_valid=s_valid, topk=topk)
    return pl.pallas_call(
        kern,
        grid=(b, t // tq),
        in_specs=[pl.BlockSpec((1, H_IDX * D_IDX, tq), lambda bi, i: (bi, 0, i)),
                  pl.BlockSpec((1, H_IDX, tq), lambda bi, i: (bi, 0, i)),
                  pl.BlockSpec((1, s, D_IDX), lambda bi, i: (bi, 0, 0)),
                  pl.BlockSpec((1, H_A * LANE, tq), lambda bi, i: (bi, 0, i)),
                  pl.BlockSpec((1, s, W_A), lambda bi, i: (bi, 0, 0)),
                  pl.BlockSpec((1, s // kb, W_A, kb), lambda bi, i: (bi, 0, 0, 0)),
                  pl.BlockSpec((kb, kb), lambda bi, i: (0, 0))],
        out_specs=pl.BlockSpec((1, tq, W_A), lambda bi, i: (bi, i, 0)),
        out_shape=jax.ShapeDtypeStruct((b, t, W_A), BF16),
        scratch_shapes=[pltpu.VMEM((s // kb, kb, tq), jnp.int32),
                        pltpu.VMEM((W_A, tq), F32),
                        pltpu.VMEM((H_A, tq), F32),
                        pltpu.VMEM((H_A, tq), F32)],
        compiler_params=_cparams(("parallel", "arbitrary")),
        name="dsa_attention",
    )(qiT, wiT, ki, qTz, k, vT4, tri)


def _hgrn_kernel(q_ref, f_ref, i_ref, g_ref, loglb_ref, log1mlb_ref, omlb_ref, gn_ref, s0_ref,
                 y_ref, sout_ref, st_ref, *, tblk):
    tb = pl.program_id(2)

    @pl.when(tb == 0)
    def _():
        st_ref[...] = s0_ref[0, 0].T

    c = CHUNK
    r = lax.broadcasted_iota(jnp.int32, (c, c), 0)
    col = lax.broadcasted_iota(jnp.int32, (c, c), 1)
    tril = r >= col
    tril_f = jnp.where(tril, 1.0, 0.0).astype(F32)
    log_lb = loglb_ref[...]
    log_1mlb = log1mlb_ref[...]
    one_m_lb = omlb_ref[...]
    gn = gn_ref[...]

    for ci in range(tblk // c):
        sl = slice(ci * c, (ci + 1) * c)
        fp = f_ref[sl, :]
        ls = jnp.minimum(fp, 0.0) - jnp.log1p(jnp.exp(-jnp.abs(fp)))
        c2 = log_1mlb + ls
        mx = jnp.maximum(log_lb, c2)
        logf = mx + jnp.log1p(jnp.exp(-jnp.abs(log_lb - c2)))
        kk = one_m_lb / (1.0 + jnp.exp(fp))
        bq = q_ref[sl, :]
        qq = bq / (1.0 + jnp.exp(-bq)) * (DK_B ** -0.5)
        vv = i_ref[sl, :]
        bcum = jnp.dot(tril_f, logf, preferred_element_type=F32,
                       precision=lax.Precision.HIGHEST)
        b_last = bcum[c - 1:c, :]
        qd = (qq * jnp.exp(bcum)).astype(BF16)
        kd = (kk * jnp.exp(-bcum)).astype(BF16)
        st = st_ref[...]
        o_inter = lax.dot_general(qd, st.astype(BF16), (((1,), (1,)), ((), ())),
                                  preferred_element_type=F32)
        attn = lax.dot_general(qd, kd, (((1,), (1,)), ((), ())), preferred_element_type=F32)
        attn = jnp.where(tril, attn, 0.0).astype(BF16)
        vb = vv.astype(BF16)
        o = o_inter + jnp.dot(attn, vb, preferred_element_type=F32)
        kl = (kk * jnp.exp(b_last - bcum)).astype(BF16)
        st_ref[...] = st * jnp.exp(b_last) + lax.dot_general(
            vb, kl, (((0,), (0,)), ((), ())), preferred_element_type=F32)
        ms = jnp.mean(o * o, axis=-1, keepdims=True)
        gate = g_ref[sl, :]
        y = o * lax.rsqrt(ms + EPS) * gn * (gate / (1.0 + jnp.exp(-gate)))
        y_ref[sl, :] = y.astype(y_ref.dtype)

    @pl.when(tb == pl.num_programs(2) - 1)
    def _():
        sout_ref[0, 0] = st_ref[...].T


def hgrn(z, col_q, col_f, col_i, col_g, row0, nb, t, log_lb, log_1mlb, one_m_lb, gn, s0, *, tblk):
    nt = t // tblk
    rb0 = row0 // tblk

    def zspec(col):
        cb = col // DK_B
        return pl.BlockSpec((tblk, DK_B), lambda b, h, tb: (rb0 + b * nt + tb, cb + h))

    pspec = pl.BlockSpec((1, DK_B), lambda b, h, tb: (0, h))
    return pl.pallas_call(
        functools.partial(_hgrn_kernel, tblk=tblk),
        grid=(nb, H_B, nt),
        in_specs=[zspec(col_q), zspec(col_f), zspec(col_i), zspec(col_g),
                  pspec, pspec, pspec,
                  pl.BlockSpec((1, DV_B), lambda b, h, tb: (0, 0)),
                  pl.BlockSpec((1, 1, DK_B, DV_B), lambda b, h, tb: (b, h, 0, 0))],
        out_specs=[pl.BlockSpec((tblk, DV_B), lambda b, h, tb: (b * nt + tb, h)),
                   pl.BlockSpec((1, 1, DK_B, DV_B), lambda b, h, tb: (b, h, 0, 0))],
        out_shape=[jax.ShapeDtypeStruct((nb * t, W_B), BF16),
                   jax.ShapeDtypeStruct((nb, H_B, DK_B, DV_B), F32)],
        scratch_shapes=[pltpu.VMEM((DV_B, DK_B), F32)],
        compiler_params=_cparams(("parallel", "parallel", "arbitrary")),
        name="hgrn",
    )(z, z, z, z, log_lb, log_1mlb, one_m_lb, gn, s0)


def _merge_kernel(ya_ref, yb_ref, yc_ref, ga_ref, gb_ref, gc_ref, x_ref,
                  wa_ref, wb_ref, wc_ref, wo_ref, o_ref):
    def sig(v):
        return 1.0 / (1.0 + jnp.exp(-v))

    merged = (sig(ga_ref[...]) * jnp.dot(ya_ref[...], wa_ref[...], preferred_element_type=F32)
              + sig(gb_ref[...]) * jnp.dot(yb_ref[...], wb_ref[...], preferred_element_type=F32)
              + sig(gc_ref[...]) * jnp.dot(yc_ref[...], wc_ref[...], preferred_element_type=F32))
    o_ref[...] = x_ref[...] + jnp.dot(merged.astype(BF16), wo_ref[...], preferred_element_type=F32)


def merge_out(ya, yb, yc, z, col_gates, x, wa, wb, wc, wo, *, tm):
    m, d = x.shape
    gb0 = col_gates // d

    def yspec(w):
        return pl.BlockSpec((tm, w), lambda i: (i, 0))

    def gspec(k):
        return pl.BlockSpec((tm, d), lambda i: (i, gb0 + k))

    def wspec(w):
        return pl.BlockSpec((w, d), lambda i: (0, 0))

    return pl.pallas_call(
        _merge_kernel,
        grid=(m // tm,),
        in_specs=[yspec(W_A), yspec(W_B), yspec(yc.shape[1]), gspec(0), gspec(1), gspec(2),
                  pl.BlockSpec((tm, d), lambda i: (i, 0)),
                  wspec(W_A), wspec(W_B), wspec(wc.shape[0]), wspec(d)],
        out_specs=pl.BlockSpec((tm, d), lambda i: (i, 0)),
        out_shape=jax.ShapeDtypeStruct((m, d), F32),
        compiler_params=_cparams(("parallel",)),
        name="merge_out",
    )(ya, yb, yc, z, z, z, x, wa, wb, wc, wo)


def _mlp_kernel(x_ref, g_ref, wu_ref, wd_ref, o_ref, h_ref, acc_ref):
    j = pl.program_id(1)

    @pl.when(j == 0)
    def _():
        x = x_ref[...]
        ms = jnp.mean(x * x, axis=-1, keepdims=True)
        h_ref[...] = (x * lax.rsqrt(ms + EPS) * g_ref[...]).astype(BF16)
        acc_ref[...] = x

    u = jnp.maximum(jnp.dot(h_ref[...], wu_ref[...], preferred_element_type=F32), 0.0)
    acc_ref[...] += jnp.dot((u * u).astype(BF16), wd_ref[...], preferred_element_type=F32)

    @pl.when(j == pl.num_programs(1) - 1)
    def _():
        o_ref[...] = acc_ref[...]


def mlp(x, g, wu, wd, *, tm, tf):
    m, d = x.shape
    f = wu.shape[1]
    return pl.pallas_call(
        _mlp_kernel,
        grid=(m // tm, f // tf),
        in_specs=[pl.BlockSpec((tm, d), lambda i, j: (i, 0)),
                  pl.BlockSpec((1, d), lambda i, j: (0, 0)),
                  pl.BlockSpec((d, tf), lambda i, j: (0, j)),
                  pl.BlockSpec((tf, d), lambda i, j: (j, 0))],
        out_specs=pl.BlockSpec((tm, d), lambda i, j: (i, 0)),
        out_shape=jax.ShapeDtypeStruct((m, d), F32),
        scratch_shapes=[pltpu.VMEM((tm, d), BF16), pltpu.VMEM((tm, d), F32)],
        compiler_params=_cparams(("parallel", "arbitrary")),
        name="mlp",
    )(x, g, wu, wd)


def _head_rms(x, g):
    return x * lax.rsqrt(jnp.mean(x * x, axis=-1, keepdims=True) + EPS) * g


def _rope(x, cos, sin):
    half = x.shape[-1] // 2
    x1, x2 = x[..., :half], x[..., half:]
    return jnp.concatenate([x1 * cos - x2 * sin, x2 * cos + x1 * sin], axis=-1)


def _pad_pairs(qT, b, t):
    q4 = qT.reshape(b, H_A // 2, 2, HD_A, t)
    z = jnp.zeros_like(q4[:, :, 0])
    even = jnp.concatenate([q4[:, :, 0], z], axis=2)
    odd = jnp.concatenate([z, q4[:, :, 1]], axis=2)
    return jnp.stack([even, odd], axis=2).reshape(b, H_A * 2 * HD_A, t)


def _attention_inputs(q, qi, wi, k, v, ki, kb):
    b, t = q.shape[0], q.shape[1]
    s = k.shape[1]
    qiT = qi.reshape(b, t, H_IDX * D_IDX).swapaxes(1, 2).astype(BF16)
    wiT = (wi * (H_IDX ** -0.5) * (D_IDX ** -0.5)).swapaxes(1, 2)
    qT = (q * (HD_A ** -0.5)).reshape(b, t, W_A).swapaxes(1, 2).astype(BF16)
    qTz = _pad_pairs(qT, b, t)
    kf = k.reshape(b, s, W_A).astype(BF16)
    vT4 = v.reshape(b, s // kb, kb, W_A).swapaxes(2, 3).astype(BF16)
    return qiT, wiT, ki.astype(BF16), qTz, kf, vT4


def kernel(x_prompt, x_sample, cache_k, cache_v, cache_kidx, state_hgrn, state_conv, norm1_g, w_in, q_norm_g, k_norm_g, hgrn_lb_logits, hgrn_gnorm_g, conv_w, w_br_a, w_br_b, w_br_c, w_out, norm2_g, w_up, w_down):
    depth = w_in.shape[0]
    bp, tp, d = x_prompt.shape
    bs, ts, _ = x_sample.shape
    past = cache_k.shape[2]
    w_c = conv_w.shape[-1]
    mp, msamp = bp * tp, bs * ts
    m = mp + msamp

    seg = dict(aq=W_A, ak=W_A, av=W_A, iq=H_IDX * D_IDX, ik=D_IDX, iw=H_IDX,
               bq=H_B * DK_B, bf=H_B * DK_B, bi=W_B, bg=W_B, cb=w_c, cc=w_c, cx=w_c, gates=3 * d)
    src, o = {}, 0
    for name in ("aq", "ak", "av", "iq", "ik", "iw", "bq", "bf", "bi", "bg", "cb", "cc", "cx", "gates"):
        src[name] = o
        o += seg[name]
    order = ("gates", "aq", "ak", "av", "bq", "bf", "bi", "bg", "cb", "cc", "cx", "iq", "ik", "iw")
    col, o = {}, 0
    for name in order:
        col[name] = o
        o += seg[name]
    tn = 512
    n_pad = -(-o // tn) * tn
    w_in_p = jnp.concatenate(
        [w_in[:, :, src[n]:src[n] + seg[n]] for n in order]
        + [jnp.zeros((depth, d, n_pad - o), w_in.dtype)], axis=-1).astype(BF16)

    lb_cum = jnp.cumsum(jax.nn.softmax(hgrn_lb_logits.astype(F32), axis=0), axis=0)
    lb = lb_cum - lb_cum[:1]
    log_lb, log_1mlb, one_m_lb = jnp.log(lb), jnp.log1p(-lb), 1.0 - lb

    half = HD_A // 2
    inv = ROPE_THETA ** (-jnp.arange(half, dtype=F32) / half)
    pos_p = jnp.arange(tp)
    pos_s = past + jnp.arange(ts)

    def tables(pos):
        ang = pos.astype(F32)[:, None] * inv[None, :]
        return jnp.cos(ang), jnp.sin(ang)

    cos_p, sin_p = tables(pos_p)
    cos_s, sin_s = tables(pos_s)

    wa_b, wb_b, wc_b = w_br_a.astype(BF16), w_br_b.astype(BF16), w_br_c.astype(BF16)
    wo_b, wu_b, wd_b = w_out.astype(BF16), w_up.astype(BF16), w_down.astype(BF16)

    x = jnp.concatenate([x_prompt.reshape(mp, d), x_sample.reshape(msamp, d)], axis=0)
    topk_p = min(TOPK_MAX, tp // 4)
    topk_s = min(TOPK_MAX, (past + ts) // 4)
    kb_p, kb_s, tq = 256, 128, 128
    s_all = past + ts
    s_pad = -(-s_all // kb_s) * kb_s
    new_p, new_s = [], []

    for l in range(depth):
        z = norm_matmul(x, norm1_g[l][None], w_in_p[l], tm=512, tn=tn)

        def cols(name, rows):
            return z[rows, col[name]:col[name] + seg[name]]

        def mixer_a(rows, b, t, cos, sin):
            c4, s4 = cos[None, :, None, :], sin[None, :, None, :]
            q = _rope(_head_rms(cols("aq", rows).reshape(b, t, H_A, HD_A), q_norm_g[l]), c4, s4)
            k = _rope(_head_rms(cols("ak", rows).reshape(b, t, H_A, HD_A), k_norm_g[l]), c4, s4)
            v = cols("av", rows).reshape(b, t, H_A, HD_A)
            qi = _rope(cols("iq", rows).reshape(b, t, H_IDX, D_IDX), c4, s4)
            ki = _rope(cols("ik", rows).reshape(b, t, D_IDX), cos[None], sin[None])
            wi = cols("iw", rows).reshape(b, t, H_IDX)
            return q, k, v, qi, ki, wi

        rows_p, rows_s = slice(0, mp), slice(mp, m)

        q, k, v, qi, ki, wi = mixer_a(rows_p, bp, tp, cos_p, sin_p)
        ya_p = dsa_attention(*_attention_inputs(q, qi, wi, k, v, ki, kb_p),
                             q_pos0=0, s_valid=tp, topk=topk_p, tq=tq, kb=kb_p)
        st_p = [k, v, ki]

        q, k, v, qi, ki, wi = mixer_a(rows_s, bs, ts, cos_s, sin_s)
        st_s = [k, v, ki]
        padq = lambda a: jnp.pad(a, ((0, 0), (0, tq - ts)) + ((0, 0),) * (a.ndim - 2))
        pads = lambda a: jnp.pad(a, ((0, 0), (0, s_pad - s_all)) + ((0, 0),) * (a.ndim - 2))
        k_all = pads(jnp.concatenate([cache_k[l], k], axis=1))
        v_all = pads(jnp.concatenate([cache_v[l], v], axis=1))
        ki_all = pads(jnp.concatenate([cache_kidx[l], ki], axis=1))
        ya_s = dsa_attention(*_attention_inputs(padq(q), padq(qi), padq(wi), k_all, v_all, ki_all, kb_s),
                             q_pos0=past, s_valid=s_all, topk=topk_s, tq=tq, kb=kb_s)[:, :ts]
        ya = jnp.concatenate([ya_p.reshape(mp, W_A), ya_s.reshape(msamp, W_A)], axis=0)

        hargs = (log_lb[l][None], log_1mlb[l][None], one_m_lb[l][None], hgrn_gnorm_g[l][None])
        yb_p, s_p = hgrn(z, col["bq"], col["bf"], col["bi"], col["bg"], 0, bp, tp, *hargs,
                         jnp.zeros((bp, H_B, DK_B, DV_B), F32), tblk=512)
        yb_s, s_s = hgrn(z, col["bq"], col["bf"], col["bi"], col["bg"], mp, bs, ts, *hargs,
                         state_hgrn[l].astype(F32), tblk=ts)
        yb = jnp.concatenate([yb_p, yb_s], axis=0)
        st_p.append(s_p)
        st_s.append(s_s)

        def mixer_c(rows, b, t, buf):
            u = (cols("cc", rows) * cols("cx", rows)).reshape(b, t, w_c)
            u_pad = jnp.concatenate([buf, u], axis=1)
            conv = sum(u_pad[:, j:j + t] * conv_w[l, j] for j in range(CONV_W))
            return (cols("cb", rows).reshape(b, t, w_c) * conv).reshape(b * t, w_c), u_pad[:, t:]

        yc_p, buf_p = mixer_c(rows_p, bp, tp, jnp.zeros((bp, CONV_W - 1, w_c), F32))
        yc_s, buf_s = mixer_c(rows_s, bs, ts, state_conv[l])
        yc = jnp.concatenate([yc_p, yc_s], axis=0).astype(BF16)
        st_p.append(buf_p)
        st_s.append(buf_s)

        x = merge_out(ya, yb, yc, z, col["gates"], x, wa_b[l], wb_b[l], wc_b[l], wo_b[l], tm=512)
        x = mlp(x, norm2_g[l][None], wu_b[l], wd_b[l], tm=512, tf=1024)
        new_p.append(st_p)
        new_s.append(st_s)

    def stack(lst, i):
        return jnp.stack([s[i] for s in lst], axis=0)

    return (x[:mp].reshape(bp, tp, d), x[mp:].reshape(bs, ts, d),
            stack(new_p, 0), stack(new_p, 1), stack(new_p, 2), stack(new_p, 3), stack(new_p, 4),
            stack(new_s, 0), stack(new_s, 1), stack(new_s, 2), stack(new_s, 3), stack(new_s, 4))
```
